```python
import functools
import jax
import jax.numpy as jnp
from jax import lax
import numpy as np

D_MODEL = 1024
BATCH = 8
SEQ = 8192
DEPTH = 1
DEC_BATCH = 128
DEC_SEQ = 4
PAST_LEN = 8192
PAGE_SIZE = 128

NORM_EPS = 1e-6
A_HEADS = 8
A_HEAD_DIM = 64
A_WIDTH = A_HEADS * A_HEAD_DIM
A_DECAY_LORA = 64
A_ICLR_LORA = 64
A_GATE_LORA = 128
A_SPLITS = (A_WIDTH, A_WIDTH, A_WIDTH, A_DECAY_LORA, A_ICLR_LORA, A_GATE_LORA)
A_COLS = sum(A_SPLITS)
A_GN_EPS = 64e-5
B_HEADS = 8
B_KV_HEADS = 2
B_HEAD_DIM = 64
B_GROUP = B_HEADS // B_KV_HEADS
ROPE_THETA = 500000.0
ROPE_DIM = B_HEAD_DIM // 4
IDX_HEADS = 4
IDX_DIM = 64
IDX_ROPE_DIM = IDX_DIM // 4
TOPK_MAX = 256
Q_BLOCK = 128
IN_SPLITS = (A_COLS, B_HEADS * B_HEAD_DIM, B_KV_HEADS * B_HEAD_DIM, B_KV_HEADS * B_HEAD_DIM,
             IDX_HEADS * IDX_DIM, IDX_DIM, IDX_HEADS, D_MODEL, D_MODEL)
IN_COLS = sum(IN_SPLITS)
N_GROUPS = 4
EXPERTS_PER_GROUP = 4
N_EXPERTS = N_GROUPS * EXPERTS_PER_GROUP
TOP_K_EXPERTS = 2
EXPERT_HIDDEN = 512
PLE_DIM = 256

kernel_name = 'rwkv7_dsa_hmoe_hybrid_step'


def split_cols(z, sizes):
    return jnp.split(z, np.cumsum(sizes)[:-1].tolist(), axis=-1)


def rms_norm(x, g):
    xf = x.astype(jnp.float32)
    y = xf * lax.rsqrt(jnp.mean(xf * xf, axis=-1, keepdims=True) + NORM_EPS)
    return (y * g.astype(jnp.float32)).astype(x.dtype)


def partial_rope(x, pos, rot_dim):
    half = rot_dim // 2
    inv_freq = ROPE_THETA ** (-jnp.arange(half, dtype=jnp.float32) / half)
    ang = pos[:, None] * inv_freq[None, :]
    cos = jnp.cos(ang)[:, None, :]
    sin = jnp.sin(ang)[:, None, :]
    xf = x.astype(jnp.float32)
    x1, x2, rest = xf[..., :half], xf[..., half:rot_dim], xf[..., rot_dim:]
    out = jnp.concatenate([x1 * cos - x2 * sin, x2 * cos + x1 * sin, rest], axis=-1)
    return out.astype(x.dtype)


def wkv_scan(s0, r, decay, k, v, a, b):
    xs = tuple(jnp.moveaxis(u, 1, 0) for u in (r, decay, k, v, a, b))

    def step(s, inp):
        r_t, w_t, k_t, v_t, a_t, b_t = inp
        sa = jnp.einsum('bhij,bhj->bhi', s, a_t)
        s = s * w_t[:, :, None, :] + sa[..., None] * b_t[:, :, None, :] + v_t[..., None] * k_t[:, :, None, :]
        y = jnp.einsum('bhij,bhj->bhi', s, r_t)
        return s, y

    s_fin, ys = lax.scan(step, s0.astype(jnp.float32), xs)
    return jnp.moveaxis(ys, 0, 1), s_fin


def rwkv7_branch(za, shift_prev, wkv_prev, lw):
    bsz, t = za.shape[:2]
    f32 = jnp.float32
    prev = jnp.concatenate([shift_prev[:, None, :].astype(za.dtype), za[:, :-1]], axis=1)
    zs = za + lw['mu_shift'] * (prev - za)
    r, k, v, d_lo, a_lo, g_lo = split_cols(zs, A_SPLITS)
    w_log = -jax.nn.softplus(-(lw['w0'] + jnp.tanh(d_lo) @ lw['w_decay_up']).astype(f32)) - 0.5
    decay = jnp.exp(-jnp.exp(w_log))
    a = jax.nn.sigmoid(lw['a0'] + a_lo @ lw['w_iclr_up'])
    g = jax.nn.sigmoid(g_lo) @ lw['w_gate_up']
    heads = lambda u: u.reshape(bsz, t, A_HEADS, A_HEAD_DIM).astype(f32)
    kk = heads(k * lw['k_k'])
    kk = kk / jnp.maximum(jnp.sqrt(jnp.sum(kk * kk, axis=-1, keepdims=True)), 1e-12)
    k = k * (1.0 + (a - 1.0) * lw['k_a'])
    rh, kh, vh, ah = heads(r), heads(k), heads(v), heads(a)
    y, wkv_new = wkv_scan(wkv_prev, rh, heads(decay), kh, vh, -kk, kk * ah)
    mean = jnp.mean(y, axis=-1, keepdims=True)
    var = jnp.mean(jnp.square(y - mean), axis=-1, keepdims=True)
    yn = (y - mean) * lax.rsqrt(var + A_GN_EPS)
    yn = yn * lw['gn_w'].reshape(A_HEADS, A_HEAD_DIM).astype(f32) + lw['gn_b'].reshape(A_HEADS, A_HEAD_DIM).astype(f32)
    bonus = jnp.sum(rh * kh * lw['r_k'].reshape(A_HEADS, A_HEAD_DIM).astype(f32), axis=-1, keepdims=True) * vh
    o = ((yn + bonus).reshape(bsz, t, A_WIDTH) * g.astype(f32)).astype(za.dtype)
    return o, wkv_new.astype(wkv_prev.dtype), za[:, -1]


def indexer_scores(qi, wi, ki):
    dots = jnp.einsum('bqhd,bsd->bqhs', qi, ki.astype(qi.dtype)).astype(jnp.float32)
    return jnp.einsum('bqhs,bqh->bqs', jax.nn.relu(dots), wi.astype(jnp.float32))


def sparse_attend(q, k_sel, v_sel, valid):
    bsz, tq = q.shape[:2]
    qg = q.reshape(bsz, tq, B_KV_HEADS, B_GROUP, B_HEAD_DIM)
    s = jnp.einsum('bqkgd,bqnkd->bqkgn', qg, k_sel.astype(q.dtype)).astype(jnp.float32) * (B_HEAD_DIM ** -0.5)
    s = jnp.where(valid[:, :, None, None, :], s, -jnp.inf)
    p = jax.nn.softmax(s, axis=-1)
    o = jnp.einsum('bqkgn,bqnkd->bqkgd', p.astype(q.dtype), v_sel.astype(q.dtype))
    return o.reshape(bsz, tq, B_HEADS * B_HEAD_DIM)


def dsa_prompt(q, k, v, qi, ki, wi):
    bsz, t = q.shape[:2]
    n_sel = min(TOPK_MAX, t // 4)
    nb = t // Q_BLOCK
    to_blocks = lambda u: jnp.moveaxis(u.reshape((bsz, nb, Q_BLOCK) + u.shape[2:]), 1, 0)
    pos_blocks = jnp.arange(t, dtype=jnp.int32).reshape(nb, Q_BLOCK)
    key_pos = jnp.arange(t, dtype=jnp.int32)
    bidx = jnp.arange(bsz)[:, None, None]

    def block(inp):
        qb, qib, wib, pb = inp
        sc = indexer_scores(qib, wib, ki)
        sc = jnp.where(key_pos[None, None, :] <= pb[None, :, None], sc, -jnp.inf)
        _, idx = lax.top_k(sc, n_sel)
        valid = idx <= pb[None, :, None]
        return sparse_attend(qb, k[bidx, idx], v[bidx, idx], valid)

    out = lax.map(block, (to_blocks(q), to_blocks(qi), to_blocks(wi), pos_blocks))
    return jnp.moveaxis(out, 0, 1).reshape(bsz, t, B_HEADS * B_HEAD_DIM)


def dsa_sample(q, k_new, v_new, qi, ki_new, wi, cache_k, cache_v, cache_ki, page_table):
    dbsz, ds = q.shape[:2]
    n_pages = page_table.shape[1]
    past = n_pages * PAGE_SIZE
    n_sel = min(TOPK_MAX, (past + ds) // 4)
    ki_past = cache_ki[page_table].reshape(dbsz, past, IDX_DIM)
    ki_all = jnp.concatenate([ki_past.astype(ki_new.dtype), ki_new], axis=1)
    sc = indexer_scores(qi, wi, ki_all)
    key_pos = jnp.arange(past + ds, dtype=jnp.int32)
    q_pos = past + jnp.arange(ds, dtype=jnp.int32)
    sc = jnp.where(key_pos[None, None, :] <= q_pos[None, :, None], sc, -jnp.inf)
    _, idx = lax.top_k(sc, n_sel)
    valid = idx <= q_pos[None, :, None]
    in_past = (idx < past)[..., None, None]
    pidx = jnp.minimum(idx, past - 1)
    phys = jnp.take_along_axis(page_table, (pidx // PAGE_SIZE).reshape(dbsz, -1), axis=1).reshape(pidx.shape)
    off = pidx % PAGE_SIZE
    nidx = jnp.clip(idx - past, 0, ds - 1)
    bidx = jnp.arange(dbsz)[:, None, None]
    k_sel = jnp.where(in_past, cache_k[phys, off].astype(k_new.dtype), k_new[bidx, nidx])
    v_sel = jnp.where(in_past, cache_v[phys, off].astype(v_new.dtype), v_new[bidx, nidx])
    return sparse_attend(q, k_sel, v_sel, valid)


def hier_moe(h, lw):
    shp = h.shape
    t = h.reshape(-1, D_MODEL)
    n = t.shape[0]
    grp_logits = (t @ lw['w_grp'] + lw['b_grp']).astype(jnp.float32)
    grp_prob = jax.nn.softmax(grp_logits, axis=-1)
    _, grp_sel = lax.top_k(grp_logits, 1)
    grp_w = jnp.take_along_axis(grp_prob, grp_sel, axis=1)
    exp_logits = (t @ lw['w_exp'] + lw['b_exp']).astype(jnp.float32).reshape(n, N_GROUPS, EXPERTS_PER_GROUP)
    sel_idx = jnp.broadcast_to(grp_sel[:, :, None], (n, 1, EXPERTS_PER_GROUP))
    in_grp = jnp.take_along_axis(exp_logits, sel_idx, axis=1)[:, 0]
    top_v, top_i = lax.top_k(in_grp, TOP_K_EXPERTS)
    top_w = jax.nn.softmax(top_v, axis=-1) * grp_w
    eid = grp_sel * EXPERTS_PER_GROUP + top_i
    gate = jnp.sum(jax.nn.one_hot(eid, N_EXPERTS, dtype=jnp.float32) * top_w[..., None], axis=1)
    out = jnp.zeros((n, D_MODEL), jnp.float32)
    for e in range(N_EXPERTS):
        hid = jax.nn.silu(t @ lw['w_gate'][e]) * (t @ lw['w_up'][e])
        out = out + gate[:, e:e + 1] * (hid @ lw['w_down'][e]).astype(jnp.float32)
    return out.astype(h.dtype).reshape(shp)


def trunk_layer(x, pemb, pos, attend, shift_prev, wkv_prev, lw):
    bsz, t = x.shape[:2]
    h = rms_norm(x, lw['g_mix'])
    z = h @ lw['w_in']
    za, zq, zk, zv, zqi, zki, zwi, zga, zgb = split_cols(z, IN_SPLITS)
    o_a, wkv_new, shift_new = rwkv7_branch(za, shift_prev, wkv_prev, lw)
    q = partial_rope(rms_norm(zq.reshape(bsz, t, B_HEADS, B_HEAD_DIM), lw['q_norm']), pos, ROPE_DIM)
    k = partial_rope(rms_norm(zk.reshape(bsz, t, B_KV_HEADS, B_HEAD_DIM), lw['k_norm']), pos, ROPE_DIM)
    v = zv.reshape(bsz, t, B_KV_HEADS, B_HEAD_DIM)
    qi = partial_rope(zqi.reshape(bsz, t, IDX_HEADS, IDX_DIM), pos, IDX_ROPE_DIM) * (IDX_DIM ** -0.5)
    ki = partial_rope(zki[:, :, None, :], pos, IDX_ROPE_DIM)[:, :, 0]
    wi = zwi * (IDX_HEADS ** -0.5)
    o_b = attend(q, k, v, qi, ki, wi)
    merged = jax.nn.sigmoid(zga) * (o_a @ lw['w_a_proj']) + jax.nn.sigmoid(zgb) * (o_b @ lw['w_b_proj'])
    x = x + merged @ lw['w_out']
    x = x + hier_moe(rms_norm(x, lw['g_ffn']), lw)
    ple_gate = jax.nn.sigmoid(rms_norm(x, lw['g_ple']) @ lw['w_ple_gate'] + lw['b_ple_gate'])
    x = x + ple_gate * (pemb @ lw['w_ple_proj'])
    return x, (k, v, ki, wkv_new, shift_new)


def _normal(k, shape, scale):
    return jax.random.normal(k, shape, jnp.float32) * scale


def setup_inputs(seed: int = 0) -> dict:
    key = jax.random.key(seed)
    ks = iter(jax.random.split(key, 64))
    nk = lambda: next(ks)
    n_pages = PAST_LEN // PAGE_SIZE
    n_used = DEC_BATCH * n_pages
    n_pool = n_used + n_used // 4 + 1
    page_table = jax.random.permutation(nk(), n_pool)[:n_used].reshape(DEC_BATCH, n_pages).astype(jnp.int32)
    L = DEPTH
    return {
        'x_prompt': _normal(nk(), (BATCH, SEQ, D_MODEL), 1.0),
        'x_sample': _normal(nk(), (DEC_BATCH, DEC_SEQ, D_MODEL), 1.0),
        'p_prompt': _normal(nk(), (DEPTH, BATCH, SEQ, PLE_DIM), 1.0),
        'p_sample': _normal(nk(), (DEPTH, DEC_BATCH, DEC_SEQ, PLE_DIM), 1.0),
        'cache_k': _normal(nk(), (DEPTH, n_pool, PAGE_SIZE, B_KV_HEADS, B_HEAD_DIM), 1.0),
        'cache_v': _normal(nk(), (DEPTH, n_pool, PAGE_SIZE, B_KV_HEADS, B_HEAD_DIM), 1.0),
        'cache_k_idx': _normal(nk(), (DEPTH, n_pool, PAGE_SIZE, IDX_DIM), 1.0),
        'state_wkv': _normal(nk(), (DEPTH, DEC_BATCH, A_HEADS, A_HEAD_DIM, A_HEAD_DIM), 0.5),
        'state_shift': _normal(nk(), (DEPTH, DEC_BATCH, A_COLS), 1.0),
        'page_table': page_table,
        'g_mix': 1.0 + _normal(nk(), (L, D_MODEL), 0.05),
        'w_in': _normal(nk(), (L, D_MODEL, IN_COLS), D_MODEL ** -0.5),
        'mu_shift': jax.random.uniform(nk(), (L, A_COLS), jnp.float32),
        'w0': jax.random.uniform(nk(), (L, A_WIDTH), jnp.float32, -6.0, -1.0),
        'w_decay_up': _normal(nk(), (L, A_DECAY_LORA, A_WIDTH), 0.1),
        'a0': _normal(nk(), (L, A_WIDTH), 0.1),
        'w_iclr_up': _normal(nk(), (L, A_ICLR_LORA, A_WIDTH), 0.1),
        'w_gate_up': _normal(nk(), (L, A_GATE_LORA, A_WIDTH), A_GATE_LORA ** -0.5),
        'k_k': 0.85 + _normal(nk(), (L, A_WIDTH), 0.05),
        'k_a': 1.0 + _normal(nk(), (L, A_WIDTH), 0.05),
        'r_k': _normal(nk(), (L, A_WIDTH), 0.1),
        'gn_w': 1.0 + _normal(nk(), (L, A_WIDTH), 0.05),
        'gn_b': _normal(nk(), (L, A_WIDTH), 0.01),
        'q_norm': 1.0 + _normal(nk(), (L, B_HEAD_DIM), 0.05),
        'k_norm': 1.0 + _normal(nk(), (L, B_HEAD_DIM), 0.05),
        'w_a_proj': _normal(nk(), (L, A_WIDTH, D_MODEL), A_WIDTH ** -0.5),
        'w_b_proj': _normal(nk(), (L, B_HEADS * B_HEAD_DIM, D_MODEL), (B_HEADS * B_HEAD_DIM) ** -0.5),
        'w_out': _normal(nk(), (L, D_MODEL, D_MODEL), D_MODEL ** -0.5),
        'g_ffn': 1.0 + _normal(nk(), (L, D_MODEL), 0.05),
        'w_grp': _normal(nk(), (L, D_MODEL, N_GROUPS), D_MODEL ** -0.5),
        'b_grp': _normal(nk(), (L, N_GROUPS), 0.01),
        'w_exp': _normal(nk(), (L, D_MODEL, N_EXPERTS), D_MODEL ** -0.5),
        'b_exp': _normal(nk(), (L, N_EXPERTS), 0.01),
        'w_up': _normal(nk(), (L, N_EXPERTS, D_MODEL, EXPERT_HIDDEN), D_MODEL ** -0.5),
        'w_gate': _normal(nk(), (L, N_EXPERTS, D_MODEL, EXPERT_HIDDEN), D_MODEL ** -0.5),
        'w_down': _normal(nk(), (L, N_EXPERTS, EXPERT_HIDDEN, D_MODEL), EXPERT_HIDDEN ** -0.5),
        'g_ple': 1.0 + _normal(nk(), (L, D_MODEL), 0.05),
        'w_ple_gate': _normal(nk(), (L, D_MODEL, D_MODEL), D_MODEL ** -0.5),
        'b_ple_gate': _normal(nk(), (L, D_MODEL), 0.01),
        'w_ple_proj': _normal(nk(), (L, PLE_DIM, D_MODEL), PLE_DIM ** -0.5),
    }


def reference(x_prompt, x_sample, p_prompt, p_sample, cache_k, cache_v, cache_k_idx, state_wkv,
              state_shift, page_table, g_mix, w_in, mu_shift, w0, w_decay_up, a0, w_iclr_up,
              w_gate_up, k_k, k_a, r_k, gn_w, gn_b, q_norm, k_norm, w_a_proj, w_b_proj, w_out,
              g_ffn, w_grp, b_grp, w_exp, b_exp, w_up, w_gate, w_down, g_ple, w_ple_gate,
              b_ple_gate, w_ple_proj):
    bsz, seq = x_prompt.shape[:2]
    ds = x_sample.shape[1]
    past = page_table.shape[1] * PAGE_SIZE
    pos_p = jnp.arange(seq, dtype=jnp.float32)
    pos_s = past + jnp.arange(ds, dtype=jnp.float32)
    hp, hs = x_prompt, x_sample
    new_p, new_s = [], []
    for i in range(DEPTH):
        lw = dict(g_mix=g_mix[i], w_in=w_in[i], mu_shift=mu_shift[i], w0=w0[i], w_decay_up=w_decay_up[i],
                  a0=a0[i], w_iclr_up=w_iclr_up[i], w_gate_up=w_gate_up[i], k_k=k_k[i], k_a=k_a[i],
                  r_k=r_k[i], gn_w=gn_w[i], gn_b=gn_b[i], q_norm=q_norm[i], k_norm=k_norm[i],
                  w_a_proj=w_a_proj[i], w_b_proj=w_b_proj[i], w_out=w_out[i], g_ffn=g_ffn[i],
                  w_grp=w_grp[i], b_grp=b_grp[i], w_exp=w_exp[i], b_exp=b_exp[i], w_up=w_up[i],
                  w_gate=w_gate[i], w_down=w_down[i], g_ple=g_ple[i], w_ple_gate=w_ple_gate[i],
                  b_ple_gate=b_ple_gate[i], w_ple_proj=w_ple_proj[i])
        zero_shift = jnp.zeros((bsz, A_COLS), x_prompt.dtype)
        zero_wkv = jnp.zeros((bsz, A_HEADS, A_HEAD_DIM, A_HEAD_DIM), jnp.float32)
        hp, st_p = trunk_layer(hp, p_prompt[i], pos_p, dsa_prompt, zero_shift, zero_wkv, lw)
        attend_s = functools.partial(dsa_sample, cache_k=cache_k[i], cache_v=cache_v[i],
                                     cache_ki=cache_k_idx[i], page_table=page_table)
        hs, st_s = trunk_layer(hs, p_sample[i], pos_s, attend_s, state_shift[i], state_wkv[i], lw)
        new_p.append(st_p)
        new_s.append(st_s)
    k_prompt = jnp.stack([s[0] for s in new_p])
    v_prompt = jnp.stack([s[1] for s in new_p])
    k_idx_prompt = jnp.stack([s[2] for s in new_p])
    wkv_prompt = jnp.stack([s[3] for s in new_p])
    shift_prompt = jnp.stack([s[4] for s in new_p])
    k_sample = jnp.stack([s[0] for s in new_s])
    v_sample = jnp.stack([s[1] for s in new_s])
    k_idx_sample = jnp.stack([s[2] for s in new_s])
    wkv_sample = jnp.stack([s[3] for s in new_s])
    shift_sample = jnp.stack([s[4] for s in new_s])
    return (hp, hs, k_prompt, v_prompt, k_idx_prompt, wkv_prompt, shift_prompt,
            k_sample, v_sample, k_idx_sample, wkv_sample, shift_sample)
```

```python
import functools
import math

import numpy as np
import jax
import jax.numpy as jnp
from jax import lax
from jax.experimental import pallas as pl
from jax.experimental.pallas import tpu as pltpu

F32, BF16, I32 = jnp.float32, jnp.bfloat16, jnp.int32

D_MODEL = 1024
NORM_EPS = 1e-6
A_HEADS = 8
A_HEAD_DIM = 64
A_WIDTH = A_HEADS * A_HEAD_DIM
A_COLS = 3 * A_WIDTH + 64 + 64 + 128
A_GN_EPS = 64e-5
B_HEADS = 8
B_KV_HEADS = 2
B_HEAD_DIM = 64
B_GROUP = B_HEADS // B_KV_HEADS
ROPE_THETA = 500000.0
ROPE_HALF = 8
IDX_HEADS = 4
IDX_DIM = 64
TOPK_MAX = 256
Q_BLOCK = 128
PAGE_SIZE = 128
N_GROUPS = 4
EXPERTS_PER_GROUP = 4
N_EXPERTS = N_GROUPS * EXPERTS_PER_GROUP
EXPERT_HIDDEN = 512
PLE_DIM = 256

LANES = 128
KEY_GROUP = 512
WKV_CHUNK = 16
VMEM_LIMIT = 56 * 1024 * 1024
INT_MIN = -2 ** 31
NEG_BIG = -1e30

C_A, C_Q, C_K, C_V, C_QI, C_KW, C_GA, C_GB, C_END = 0, 1792, 2304, 2432, 2560, 2816, 2944, 3968, 4992


def _cparams(sem):
    return pltpu.CompilerParams(dimension_semantics=sem, vmem_limit_bytes=VMEM_LIMIT)


def _const_spec(shape):
    nd = len(shape)
    return pl.BlockSpec(shape, lambda *_: (0,) * nd)


def _sigmoid(x):
    return 1.0 / (1.0 + jnp.exp(-x))


def _div_pow2(x, d):
    return jnp.right_shift(x, int(math.log2(d)))


def _dot(a, b):
    return jnp.dot(a, b, preferred_element_type=F32)


def _dot_nt(a, b):
    return lax.dot_general(a, b, (((1,), (1,)), ((), ())), preferred_element_type=F32)


def _dot_tn(a, b):
    return lax.dot_general(a, b, (((0,), (0,)), ((), ())), preferred_element_type=F32)


def _split(x, parts):
    out = []
    for _ in range(parts - 1):
        h = x.astype(BF16)
        out.append(h)
        x = x - h.astype(F32)
    out.append(x.astype(BF16))
    return out


def _dot_data_lhs(x, m, parts=3):
    return sum(_dot(p, m) for p in _split(x, parts))


def _dot_data_rhs(m, x, parts=3):
    return sum(_dot(m, p) for p in _split(x, parts))


def _dot_x3(x, w_hi, w_lo):
    x_hi, x_lo = _split(x, 2)
    return _dot(x_hi, w_hi) + _dot(x_lo, w_hi) + _dot(x_hi, w_lo)


def _seg_sum64(x, ones_blk):
    cols = [_dot_data_lhs(x[:, j:j + LANES], ones_blk) for j in range(0, x.shape[1], LANES)]
    return cols[0] if len(cols) == 1 else jnp.concatenate(cols, axis=1)


def _rms(x, g):
    return x * lax.rsqrt(jnp.mean(x * x, axis=-1, keepdims=True) + NORM_EPS) * g


def _inproj_kernel(x_ref, g_ref, w_ref, cs_ref, qn_ref, kn_ref, ones_ref,
                   za_ref, q_ref, k_ref, v_ref, kb_ref, vb_ref, qi_ref, ki_ref, kib_ref, wi_ref,
                   ga_ref, gb_ref):
    tm = x_ref.shape[0]
    h = _rms(x_ref[...], g_ref[...]).astype(BF16)

    def proj(lo, hi):
        return _dot(h, w_ref[:, lo:hi])

    cos = cs_ref[:, :LANES]
    sin = cs_ref[:, LANES:]
    lane = lax.broadcasted_iota(I32, (tm, LANES), 1)
    first_half = (lane & 63) < ROPE_HALF
    ones_blk = ones_ref[...]

    def rope(u):
        partner = jnp.where(first_half, pltpu.roll(u, LANES - ROPE_HALF, 1), pltpu.roll(u, ROPE_HALF, 1))
        return u * cos + partner * sin

    def head_norm(u, g):
        ms = _seg_sum64(u * u, ones_blk) * (1.0 / 64.0)
        return u * lax.rsqrt(ms + NORM_EPS) * g

    za_ref[...] = proj(C_A, C_Q)

    zq = proj(C_Q, C_K)
    for j in range(0, 512, LANES):
        q_ref[:, j:j + LANES] = (rope(head_norm(zq[:, j:j + LANES], qn_ref[...])) * 0.125).astype(BF16)

    kk = rope(head_norm(proj(C_K, C_V), kn_ref[...]))
    k_ref[...] = kk
    kb_ref[...] = kk.astype(BF16)
    vv = proj(C_V, C_QI)
    v_ref[...] = vv
    vb_ref[...] = vv.astype(BF16)

    zqi = proj(C_QI, C_KW)
    for j in range(0, 256, LANES):
        qi_ref[:, j:j + LANES] = (rope(zqi[:, j:j + LANES]) * 0.125).astype(BF16)

    zkw = proj(C_KW, C_GA)
    kr = jnp.where(lane < 64, rope(zkw), 0.0)
    ki_ref[...] = kr[:, :64]
    kib_ref[...] = (kr + pltpu.roll(kr, 64, 1)).astype(BF16)
    wi_ref[...] = zkw * (IDX_HEADS ** -0.5)

    ga_ref[...] = _sigmoid(proj(C_GA, C_GB)).astype(BF16)
    gb_ref[...] = _sigmoid(proj(C_GB, C_END)).astype(BF16)


def _inproj(x2d, cs_tab, g_mix, w_pad, qn, kn, ones_blk, tm):
    n = x2d.shape[0]
    tab_blocks = cs_tab.shape[0] // tm
    row = lambda w: pl.BlockSpec((tm, w), lambda i: (i, 0))
    out_w = [(A_COLS, F32), (512, BF16), (128, F32), (128, F32), (128, BF16), (128, BF16), (256, BF16),
             (64, F32), (128, BF16), (128, F32), (D_MODEL, BF16), (D_MODEL, BF16)]
    return pl.pallas_call(
        _inproj_kernel,
        grid=(n // tm,),
        in_specs=[row(D_MODEL), _const_spec((1, D_MODEL)),
                  pl.BlockSpec((D_MODEL, C_END), lambda i: (0, 0), pipeline_mode=pl.Buffered(1)),
                  pl.BlockSpec((tm, 2 * LANES), lambda i: (i % tab_blocks, 0)),
                  _const_spec((1, LANES)), _const_spec((1, LANES)), _const_spec((LANES, LANES))],
        out_specs=[row(w) for w, _ in out_w],
        out_shape=[jax.ShapeDtypeStruct((n, w), dt) for w, dt in out_w],
        compiler_params=_cparams(("parallel",)),
        name="inproj",
    )(x2d, g_mix, w_pad, cs_tab, qn, kn, ones_blk)


def _rwkv_kernel(za_ref, shift_ref, s0_ref, mu_ref, w0_ref, lwh_ref, lwl_ref, a0_ref, wg_ref,
                 kk_ref, ka_ref, rk_ref, gnw_ref, gnb_ref, ones_ref, et_ref,
                 o_ref, s_out_ref, s_scr, prev_scr, y_scr, *, t_valid):
    tb = za_ref.shape[1]
    c = WKV_CHUNK
    hc = A_HEADS * c
    blk = pl.program_id(1)

    @pl.when(blk == 0)
    def _():
        s_scr[...] = s0_ref[0]
        prev_scr[...] = shift_ref[0]

    za = za_ref[0]
    row = lax.broadcasted_iota(I32, (tb, A_COLS), 0)
    prev_tok = jnp.where(row == 0, prev_scr[...], pltpu.roll(za, 1, 0))
    prev_scr[...] = za[tb - 1:tb, :]
    zs = za + mu_ref[...] * (prev_tok - za)

    r = zs[:, 0:512]
    k = zs[:, 512:1024]
    v = zs[:, 1024:1536]
    lane = lax.broadcasted_iota(I32, (tb, LANES), 1)
    lo = zs[:, 1536:1664]
    lo_act = jnp.where(lane < 64, jnp.tanh(lo), lo)
    la = _dot_x3(lo_act, lwh_ref[...], lwl_ref[...])
    wl = -(w0_ref[...] + la[:, :512])
    w_log = -(jnp.maximum(wl, 0.0) + jnp.log1p(jnp.exp(-jnp.abs(wl)))) - 0.5
    lw = -jnp.exp(w_log)
    a = _sigmoid(a0_ref[...] + la[:, 512:])
    g = _dot(_sigmoid(zs[:, 1664:1792]).astype(BF16), wg_ref[...])
    ones_blk = ones_ref[...]
    kkr = k * kk_ref[...]
    kk = kkr / jnp.maximum(jnp.sqrt(_seg_sum64(kkr * kkr, ones_blk)), 1e-12)
    k2 = k * (1.0 + (a - 1.0) * ka_ref[...])
    if t_valid < tb:
        ok = lax.broadcasted_iota(I32, (tb, A_WIDTH), 0) < t_valid
        lw = jnp.where(ok, lw, 0.0)
        kk = jnp.where(ok, kk, 0.0)
        k2 = jnp.where(ok, k2, 0.0)

    ri = lax.broadcasted_iota(I32, (tb, tb), 0)
    ci = lax.broadcasted_iota(I32, (tb, tb), 1)
    same = _div_pow2(ri, c) == _div_pow2(ci, c)
    tri = jnp.where(same & (ci <= ri), 1.0, 0.0).astype(BF16)
    blk1 = jnp.where(same, 1.0, 0.0).astype(BF16)
    cum = _dot_data_rhs(tri, lw)
    tot = _dot_data_rhs(blk1, lw)
    kka = kk * a
    e_out = jnp.exp(-cum)
    e_rem = jnp.exp(tot - cum)
    r_h = r * jnp.exp(cum)
    a_h = -kk * jnp.exp(cum - lw)
    b_h = kka * e_out
    k_h = k2 * e_out
    b_w = kka * e_rem
    k_w = k2 * e_rem
    w_c = jnp.exp(tot)

    pr = lax.broadcasted_iota(I32, (hc, A_WIDTH), 0)
    pc = lax.broadcasted_iota(I32, (hc, A_WIDTH), 1)
    head_mask = _div_pow2(pr, c) == _div_pow2(pc, A_HEAD_DIM)
    mr = lax.broadcasted_iota(I32, (hc, hc), 0)
    mc = lax.broadcasted_iota(I32, (hc, hc), 1)
    same_h = _div_pow2(mr, c) == _div_pow2(mc, c)
    strict = same_h & (mc < mr)
    incl = same_h & (mc <= mr)
    eye = jnp.where(mr == mc, 1.0, 0.0)
    sel = jnp.where((lax.broadcasted_iota(I32, (c, hc), 1) & (c - 1)) == lax.broadcasted_iota(I32, (c, hc), 0),
                    1.0, 0.0).astype(BF16)
    et = et_ref[...]

    def packed(x, j):
        return jnp.where(head_mask, jnp.tile(x[j * c:(j + 1) * c], (A_HEADS, 1)), 0.0).astype(BF16)

    n_sq = int(math.log2(c)) - 1
    for j in range(tb // c):
        lhs = jnp.concatenate([packed(a_h, j), packed(r_h, j)], axis=0)
        rhs = jnp.concatenate([packed(b_h, j), packed(k_h, j)], axis=0)
        aa = _dot_nt(lhs, rhs)
        a_ab = jnp.where(strict, aa[:hc, :hc], 0.0)
        a_ak = jnp.where(strict, aa[:hc, hc:], 0.0)
        a_rb = jnp.where(incl, aa[hc:, :hc], 0.0)
        a_rk = jnp.where(incl, aa[hc:, hc:], 0.0)
        s = s_scr[...]
        st = _dot_nt(lhs, s.astype(BF16))
        vp = _dot_nt(packed(v, j), et).astype(BF16)
        inv = eye + a_ab
        lp = a_ab
        for _ in range(n_sq):
            lpb = lp.astype(BF16)
            lp = _dot(lpb, lpb)
            inv = inv + _dot(inv.astype(BF16), lp.astype(BF16))
        u = _dot(inv.astype(BF16), (st[:hc] + _dot(a_ak.astype(BF16), vp)).astype(BF16))
        uv = jnp.concatenate([u.astype(BF16), vp], axis=0)
        y = st[hc:] + _dot(jnp.concatenate([a_rb, a_rk], axis=1).astype(BF16), uv)
        bk_w = jnp.concatenate([packed(b_w, j), packed(k_w, j)], axis=0)
        s_scr[...] = s * w_c[j * c:j * c + 1, :] + _dot_tn(uv, bk_w)
        y_wide = jnp.where(head_mask, _dot_data_lhs(y, et, 2), 0.0)
        y_scr[j * c:(j + 1) * c, :] = _dot_data_rhs(sel, y_wide, 2)

    y = y_scr[...]
    mean = _seg_sum64(y, ones_blk) * (1.0 / 64.0)
    d = y - mean
    var = _seg_sum64(d * d, ones_blk) * (1.0 / 64.0)
    yn = d * lax.rsqrt(var + A_GN_EPS) * gnw_ref[...] + gnb_ref[...]
    bonus = _seg_sum64(r * k2 * rk_ref[...], ones_blk) * v
    o_ref[0] = ((yn + bonus) * g).astype(BF16)

    @pl.when(blk == pl.num_programs(1) - 1)
    def _():
        s_out_ref[0] = s_scr[...]


def _rwkv(za3, shift_prev, s0, p, tb, t_valid):
    bsz, t, _ = za3.shape
    vec = lambda w: _const_spec((1, w))
    in_specs = [pl.BlockSpec((1, tb, A_COLS), lambda b, i: (b, i, 0)),
                pl.BlockSpec((1, 1, A_COLS), lambda b, i: (b, 0, 0)),
                pl.BlockSpec((1, 64, A_WIDTH), lambda b, i: (b, 0, 0)),
                vec(A_COLS), vec(512), _const_spec((128, 1024)), _const_spec((128, 1024)), vec(512),
                _const_spec((128, 512)), vec(512), vec(512), vec(512), vec(512), vec(512),
                _const_spec((LANES, LANES)), _const_spec((64, A_WIDTH))]
    return pl.pallas_call(
        functools.partial(_rwkv_kernel, t_valid=t_valid),
        grid=(bsz, t // tb),
        in_specs=in_specs,
        out_specs=[pl.BlockSpec((1, tb, A_WIDTH), lambda b, i: (b, i, 0)),
                   pl.BlockSpec((1, 64, A_WIDTH), lambda b, i: (b, 0, 0))],
        out_shape=[jax.ShapeDtypeStruct((bsz, t, A_WIDTH), BF16),
                   jax.ShapeDtypeStruct((bsz, 64, A_WIDTH), F32)],
        scratch_shapes=[pltpu.VMEM((64, A_WIDTH), F32), pltpu.VMEM((1, A_COLS), F32),
                        pltpu.VMEM((tb, A_WIDTH), F32)],
        compiler_params=_cparams(("parallel", "arbitrary")),
        name="rwkv7",
    )(za3, shift_prev, s0, p["mu"], p["w0"], p["lora_hi"], p["lora_lo"], p["a0"], p["wg"],
      p["k_k"], p["k_a"], p["r_k"], p["gn_w"], p["gn_b"], p["ones_blk"], p["et"])


def _sort_key(sc):
    bits = pltpu.bitcast(sc + 0.0, I32)
    return jnp.where(bits < 0, bits ^ 0x7FFFFFFF, bits)


def _place_heads(x, rows, want_half):
    lane = lax.broadcasted_iota(I32, (rows, LANES), 1)
    out = []
    for h in range(x.shape[1] // 64):
        col = x[:, (h // 2) * LANES:(h // 2 + 1) * LANES]
        if h % 2 != want_half(h):
            col = pltpu.roll(col, 64, 1)
        keep = (lane < 64) if want_half(h) == 0 else (lane >= 64)
        out.append(jnp.where(keep, col, 0.0))
    return jnp.concatenate(out, axis=0)


def _gather_heads(acc, rows):
    lane = lax.broadcasted_iota(I32, (rows, LANES), 1)
    cols = []
    for j in range(B_HEADS // 2):
        parts = []
        for h in (2 * j, 2 * j + 1):
            blk = acc[h * rows:(h + 1) * rows, :]
            if (h // B_GROUP) != (h % 2):
                blk = pltpu.roll(blk, 64, 1)
            parts.append(blk)
        cols.append(jnp.where(lane < 64, parts[0], parts[1]))
    return jnp.concatenate(cols, axis=1)


def _flash_init(m_scr, l_scr, acc_scr):
    m_scr[...] = jnp.full(m_scr.shape, NEG_BIG, F32)
    l_scr[...] = jnp.zeros(l_scr.shape, F32)
    acc_scr[...] = jnp.zeros(acc_scr.shape, F32)


def _flash_step(qp, kt, vt, msk, m_scr, l_scr, acc_scr):
    s = _dot_nt(qp, kt)
    on = msk > 0.0
    m_old = m_scr[...]
    m_new = jnp.maximum(m_old, jnp.max(jnp.where(on, s, NEG_BIG), axis=1, keepdims=True))
    alpha = jnp.exp(m_old - m_new)
    p = jnp.where(on, jnp.exp(s - m_new), 0.0)
    l_scr[...] = alpha * l_scr[...] + jnp.sum(p, axis=1, keepdims=True)
    acc_scr[...] = alpha * acc_scr[...] + _dot(p.astype(BF16), vt)
    m_scr[...] = m_new


def _dsa_prompt_kernel(q_ref, qi_ref, wt_ref, kb_ref, vb_ref, kib_ref, o_ref,
                       key_scr, tie_scr, msk_scr, m_scr, l_scr, acc_scr, *, n_sel, idx_bits):
    qb = pl.program_id(1)
    n_grp = _div_pow2(qb, KEY_GROUP // Q_BLOCK) + 1
    gk = KEY_GROUP
    kpos0 = lax.broadcasted_iota(I32, (gk, LANES), 0)
    qpos = qb * Q_BLOCK + lax.broadcasted_iota(I32, (gk, LANES), 1)

    qip = _place_heads(qi_ref[0].astype(F32), Q_BLOCK, lambda h: h % 2).astype(BF16)
    wt = wt_ref[0]

    def score_body(gi, carry):
        kt = kib_ref[0, pl.ds(pl.multiple_of(gi * gk, gk), gk), :]
        d = _dot_nt(kt, qip)
        sc = sum(jnp.maximum(d[:, h * LANES:(h + 1) * LANES], 0.0) * wt[h:h + 1, :] for h in range(IDX_HEADS))
        key_scr[gi] = jnp.where(kpos0 + gi * gk <= qpos, _sort_key(sc), INT_MIN)
        return carry

    lax.fori_loop(0, n_grp, score_body, 0)

    def count(pred_fn, ref):
        def body(gi, acc):
            c = jnp.where(pred_fn(ref[gi]), 1.0, 0.0)
            return acc + jnp.sum(c.reshape(gk // 8, 8, LANES), axis=0)
        acc = lax.fori_loop(0, n_grp, body, jnp.zeros((8, LANES), F32))
        return jnp.sum(acc, axis=0, keepdims=True)

    def value_bit(i, t):
        cand = t + jnp.left_shift(jnp.int32(1), 31 - i)
        return jnp.where(count(lambda m: m >= cand, key_scr) >= n_sel, cand, t)

    thr = lax.fori_loop(0, 32, value_bit, jnp.full((1, LANES), INT_MIN, I32))
    need = n_sel - count(lambda m: m > thr, key_scr)

    def tie_body(gi, carry):
        tie_scr[gi] = jnp.where(key_scr[gi] == thr, kpos0 + gi * gk, jnp.int32(2 ** 30))
        return carry

    lax.fori_loop(0, n_grp, tie_body, 0)

    def index_bit(i, j):
        cand = j + jnp.left_shift(jnp.int32(1), idx_bits - 1 - i)
        return jnp.where(count(lambda ti: ti < cand, tie_scr) < need, cand, j)

    jmax = lax.fori_loop(0, idx_bits, index_bit, jnp.zeros((1, LANES), I32))
    real = thr != INT_MIN

    def mask_body(gi, carry):
        chosen = (key_scr[gi] > thr) | ((tie_scr[gi] <= jmax) & real)
        msk_scr[gi] = jnp.where(chosen, 1.0, 0.0).T
        return carry

    lax.fori_loop(0, n_grp, mask_body, 0)

    qp = _place_heads(q_ref[0].astype(F32), Q_BLOCK, lambda h: h // B_GROUP).astype(BF16)
    _flash_init(m_scr, l_scr, acc_scr)

    def attn_body(gi, carry):
        sl = pl.ds(pl.multiple_of(gi * gk, gk), gk)
        msk = jnp.tile(msk_scr[gi], (B_HEADS, 1))
        _flash_step(qp, kb_ref[0, sl, :], vb_ref[0, sl, :], msk, m_scr, l_scr, acc_scr)
        return carry

    lax.fori_loop(0, n_grp, attn_body, 0)
    o_ref[0] = _gather_heads(acc_scr[...] / l_scr[...], Q_BLOCK).astype(BF16)


def _dsa_prompt(q, qi, wt, kb, vb, kib):
    bsz, t, _ = q.shape
    n_sel = min(TOPK_MAX, t // 4)
    n_grp_max = t // KEY_GROUP
    blk = lambda w: pl.BlockSpec((1, Q_BLOCK, w), lambda b, i: (b, i, 0))
    full = lambda w: pl.BlockSpec((1, t, w), lambda b, i: (b, 0, 0))
    rows = B_HEADS * Q_BLOCK
    return pl.pallas_call(
        functools.partial(_dsa_prompt_kernel, n_sel=float(n_sel), idx_bits=int(math.ceil(math.log2(t)))),
        grid=(bsz, t // Q_BLOCK),
        in_specs=[blk(512), blk(256), pl.BlockSpec((1, IDX_HEADS, Q_BLOCK), lambda b, i: (b, 0, i)),
                  full(128), full(128), full(128)],
        out_specs=blk(512),
        out_shape=jax.ShapeDtypeStruct((bsz, t, 512), BF16),
        scratch_shapes=[pltpu.VMEM((n_grp_max, KEY_GROUP, LANES), I32),
                        pltpu.VMEM((n_grp_max, KEY_GROUP, LANES), I32),
                        pltpu.VMEM((n_grp_max, Q_BLOCK, KEY_GROUP), F32),
                        pltpu.VMEM((rows, 1), F32), pltpu.VMEM((rows, 1), F32), pltpu.VMEM((rows, LANES), F32)],
        compiler_params=_cparams(("parallel", "arbitrary")),
        name="dsa_prompt",
    )(q, qi, wt, kb, vb, kib)


SAMPLE_ROWS = 8


def _dsa_sample_kernel(pt_ref, q_ref, qi_ref, wi_ref, kn_ref, vn_ref, kin_ref, ck_ref, cv_ref, cki_ref,
                       o_ref, kbuf, vbuf, kibuf, sems, m_scr, l_scr, acc_scr, *, n_sel, ds):
    b = pl.program_id(0)
    nb = pl.num_programs(0)
    n_pages = pt_ref.shape[1]
    past = n_pages * PAGE_SIZE
    gk = KEY_GROUP
    r8 = SAMPLE_ROWS

    def page_copies(bb, slot, p):
        pg = pt_ref[bb, p]
        rows = pl.ds(pl.multiple_of(p * PAGE_SIZE, PAGE_SIZE), PAGE_SIZE)
        return (pltpu.make_async_copy(ck_ref.at[pg], kbuf.at[slot, rows], sems.at[slot, 0]),
                pltpu.make_async_copy(cv_ref.at[pg], vbuf.at[slot, rows], sems.at[slot, 1]),
                pltpu.make_async_copy(cki_ref.at[pg], kibuf.at[slot, rows], sems.at[slot, 2]))

    def fetch(bb, slot):
        def body(p, carry):
            for cp in page_copies(bb, slot, p):
                cp.start()
            return carry
        lax.fori_loop(0, n_pages, body, 0)

    def wait(bb, slot):
        def body(p, carry):
            for cp in page_copies(bb, slot, p):
                cp.wait()
            return carry
        lax.fori_loop(0, n_pages, body, 0)

    slot = b & 1

    @pl.when(b == 0)
    def _():
        fetch(b, slot)

    @pl.when(b + 1 < nb)
    def _():
        fetch(b + 1, 1 - slot)

    wait(b, slot)

    qis = _place_heads(qi_ref[0].astype(F32), r8, lambda h: 0)[:, :64].astype(BF16)
    wi = wi_ref[0]
    wcol = [wi[:, 64 + h:65 + h] for h in range(IDX_HEADS)]

    def idx_scores(keys_bf16):
        d = _dot_nt(qis, keys_bf16)
        return sum(jnp.maximum(d[h * r8:(h + 1) * r8], 0.0) * wcol[h] for h in range(IDX_HEADS))

    qrow = lax.broadcasted_iota(I32, (r8, LANES), 0)
    nlane = lax.broadcasted_iota(I32, (r8, LANES), 1)
    parts = [_sort_key(idx_scores(kibuf[slot, g * gk:(g + 1) * gk, :].astype(BF16))) for g in range(past // gk)]
    new_ok = (nlane <= qrow) & (nlane < ds)
    parts.append(jnp.where(new_ok, _sort_key(idx_scores(kin_ref[0][:, :64])), INT_MIN))
    keys = jnp.concatenate(parts, axis=1)
    width = past + LANES
    pos = lax.broadcasted_iota(I32, (r8, width), 1)

    def count(pred):
        return jnp.sum(jnp.where(pred, 1.0, 0.0), axis=1, keepdims=True)

    def value_bit(i, t):
        cand = t + jnp.left_shift(jnp.int32(1), 31 - i)
        return jnp.where(count(keys >= cand) >= n_sel, cand, t)

    thr = lax.fori_loop(0, 32, value_bit, jnp.full((r8, 1), INT_MIN, I32))
    need = n_sel - count(keys > thr)
    tie_pos = jnp.where(keys == thr, pos, jnp.int32(2 ** 30))
    idx_bits = int(math.ceil(math.log2(width)))

    def index_bit(i, j):
        cand = j + jnp.left_shift(jnp.int32(1), idx_bits - 1 - i)
        return jnp.where(count(tie_pos < cand) < need, cand, j)

    jmax = lax.fori_loop(0, idx_bits, index_bit, jnp.zeros((r8, 1), I32))
    chosen = (keys > thr) | ((tie_pos <= jmax) & (thr != INT_MIN))
    msk = jnp.where(chosen, 1.0, 0.0)

    qp = _place_heads(q_ref[0].astype(F32), r8, lambda h: h // B_GROUP).astype(BF16)
    _flash_init(m_scr, l_scr, acc_scr)
    for g in range(past // gk):
        sl = slice(g * gk, (g + 1) * gk)
        _flash_step(qp, kbuf[slot, sl, :].astype(BF16), vbuf[slot, sl, :].astype(BF16),
                    jnp.tile(msk[:, sl], (B_HEADS, 1)), m_scr, l_scr, acc_scr)
    _flash_step(qp, kn_ref[0], vn_ref[0], jnp.tile(msk[:, past:], (B_HEADS, 1)), m_scr, l_scr, acc_scr)
    o_ref[0] = _gather_heads(acc_scr[...] / l_scr[...], r8).astype(BF16)


def _dsa_sample(page_table, q8, qi8, wi8, kn, vn, kin, cache_k, cache_v, cache_ki, ds):
    dbsz = q8.shape[0]
    n_pages = page_table.shape[1]
    past = n_pages * PAGE_SIZE
    n_sel = min(TOPK_MAX, (past + ds) // 4)
    r8 = SAMPLE_ROWS
    blk = lambda r, w: pl.BlockSpec((1, r, w), lambda b, pt: (b, 0, 0))
    any_spec = pl.BlockSpec(memory_space=pl.ANY)
    rows = B_HEADS * r8
    grid_spec = pltpu.PrefetchScalarGridSpec(
        num_scalar_prefetch=1,
        grid=(dbsz,),
        in_specs=[blk(r8, 512), blk(r8, 256), blk(r8, 128), blk(LANES, 128), blk(LANES, 128), blk(LANES, 128),
                  any_spec, any_spec, any_spec],
        out_specs=blk(r8, 512),
        scratch_shapes=[pltpu.VMEM((2, past, 128), F32), pltpu.VMEM((2, past, 128), F32),
                        pltpu.VMEM((2, past, IDX_DIM), F32), pltpu.SemaphoreType.DMA((2, 3)),
                        pltpu.VMEM((rows, 1), F32), pltpu.VMEM((rows, 1), F32), pltpu.VMEM((rows, LANES), F32)],
    )
    return pl.pallas_call(
        functools.partial(_dsa_sample_kernel, n_sel=float(n_sel), ds=ds),
        grid_spec=grid_spec,
        out_shape=jax.ShapeDtypeStruct((dbsz, r8, 512), BF16),
        compiler_params=_cparams(("arbitrary",)),
        name="dsa_sample",
    )(page_table, q8, qi8, wi8, kn, vn, kin, cache_k, cache_v, cache_ki)


def _merge_kernel(x_ref, oa_ref, ob_ref, ga_ref, gb_ref, wa_ref, wb_ref, wo_ref, gf_ref, wrh_ref, wrl_ref, br_ref,
                  x1_ref, h2_ref, gate_ref):
    tm = x_ref.shape[0]
    merged = (ga_ref[...].astype(F32) * _dot(oa_ref[...], wa_ref[...])
              + gb_ref[...].astype(F32) * _dot(ob_ref[...], wb_ref[...]))
    x1 = x_ref[...] + _dot(merged.astype(BF16), wo_ref[...])
    x1_ref[...] = x1
    h2 = _rms(x1, gf_ref[...])
    h2_ref[...] = h2.astype(BF16)

    logits = _dot_x3(h2, wrh_ref[...], wrl_ref[...]) + br_ref[...]
    lane_i = lax.broadcasted_iota(I32, (tm, LANES), 1)
    lane = lane_i.astype(F32)
    is_grp = lane_i < N_GROUPS
    gl = jnp.where(is_grp, logits, -jnp.inf)
    gmax = jnp.max(gl, axis=1, keepdims=True)
    gsel = jnp.min(jnp.where(gl == gmax, lane, float(LANES)), axis=1, keepdims=True)
    grp_w = 1.0 / jnp.sum(jnp.where(is_grp, jnp.exp(gl - gmax), 0.0), axis=1, keepdims=True)
    e_idx = lane_i - N_GROUPS
    lane_grp = _div_pow2(e_idx, EXPERTS_PER_GROUP).astype(F32)
    in_grp = jnp.where((e_idx >= 0) & (e_idx < N_EXPERTS), lane_grp, -1.0) == gsel
    el = jnp.where(in_grp, logits, -jnp.inf)
    v1 = jnp.max(el, axis=1, keepdims=True)
    i1 = jnp.min(jnp.where(el == v1, lane, float(LANES)), axis=1, keepdims=True)
    el2 = jnp.where(lane == i1, -jnp.inf, el)
    v2 = jnp.max(el2, axis=1, keepdims=True)
    i2 = jnp.min(jnp.where(el2 == v2, lane, float(LANES)), axis=1, keepdims=True)
    e2 = jnp.exp(v2 - v1)
    w1 = grp_w / (1.0 + e2)
    w2 = grp_w * e2 / (1.0 + e2)
    gate_ref[...] = jnp.where(lane == i1, w1, 0.0) + jnp.where(lane == i2, w2, 0.0)


def _merge(x2d, oa, ob, ga, gb, p, tm):
    n = x2d.shape[0]
    row = lambda w: pl.BlockSpec((tm, w), lambda i: (i, 0))
    return pl.pallas_call(
        _merge_kernel,
        grid=(n // tm,),
        in_specs=[row(D_MODEL), row(512), row(512), row(D_MODEL), row(D_MODEL),
                  _const_spec((512, D_MODEL)), _const_spec((512, D_MODEL)), _const_spec((D_MODEL, D_MODEL)),
                  _const_spec((1, D_MODEL)), _const_spec((D_MODEL, LANES)), _const_spec((D_MODEL, LANES)),
                  _const_spec((1, LANES))],
        out_specs=[row(D_MODEL), row(D_MODEL), row(LANES)],
        out_shape=[jax.ShapeDtypeStruct((n, D_MODEL), F32), jax.ShapeDtypeStruct((n, D_MODEL), BF16),
                   jax.ShapeDtypeStruct((n, LANES), F32)],
        compiler_params=_cparams(("parallel",)),
        name="merge_router",
    )(x2d, oa, ob, ga, gb, p["wa"], p["wb"], p["wo"], p["g_ffn"], p["wr_hi"], p["wr_lo"], p["b_router"])


def _moe_kernel(h2_ref, gate_ref, x1_ref, pe_ref, wg_ref, wu_ref, wd_ref, gp_ref, wpg_ref, bpg_ref, wpp_ref,
                y_ref, acc_scr):
    e = pl.program_id(1)
    tm = h2_ref.shape[0]

    @pl.when(e == 0)
    def _():
        acc_scr[...] = jnp.zeros(acc_scr.shape, F32)

    h2 = h2_ref[...]
    a = _dot(h2, wg_ref[0])
    hid = a * _sigmoid(a) * _dot(h2, wu_ref[0])
    lane = lax.broadcasted_iota(I32, (tm, LANES), 1)
    ge = jnp.sum(jnp.where(lane == e + N_GROUPS, gate_ref[...], 0.0), axis=1, keepdims=True)
    acc_scr[...] += ge * _dot(hid.astype(BF16), wd_ref[0])

    @pl.when(e == pl.num_programs(1) - 1)
    def _():
        x2 = x1_ref[...] + acc_scr[...]
        hp = _rms(x2, gp_ref[...]).astype(BF16)
        pg = _sigmoid(_dot(hp, wpg_ref[...]) + bpg_ref[...])
        y_ref[...] = x2 + pg * _dot(pe_ref[...].astype(BF16), wpp_ref[...])


def _moe(h2, gate, x1, pemb, p, tm):
    n = h2.shape[0]
    row = lambda w: pl.BlockSpec((tm, w), lambda i, e: (i, 0))
    ew = lambda a, b: pl.BlockSpec((1, a, b), lambda i, e: (e, 0, 0))
    return pl.pallas_call(
        _moe_kernel,
        grid=(n // tm, N_EXPERTS),
        in_specs=[row(D_MODEL), row(LANES), row(D_MODEL), row(PLE_DIM),
                  ew(D_MODEL, EXPERT_HIDDEN), ew(D_MODEL, EXPERT_HIDDEN), ew(EXPERT_HIDDEN, D_MODEL),
                  _const_spec((1, D_MODEL)), _const_spec((D_MODEL, D_MODEL)), _const_spec((1, D_MODEL)),
                  _const_spec((PLE_DIM, D_MODEL))],
        out_specs=row(D_MODEL),
        out_shape=jax.ShapeDtypeStruct((n, D_MODEL), F32),
        scratch_shapes=[pltpu.VMEM((tm, D_MODEL), F32)],
        compiler_params=_cparams(("parallel", "arbitrary")),
        name="moe_ple",
    )(h2, gate, x1, pemb, p["w_gate"], p["w_up"], p["w_down"], p["g_ple"], p["w_ple_gate"], p["b_ple_gate"],
      p["w_ple_proj"])


def _hi_lo(w):
    hi = w.astype(BF16)
    return hi, (w - hi.astype(F32)).astype(BF16)


def _rope_table(pos):
    inv_freq = ROPE_THETA ** (-jnp.arange(ROPE_HALF, dtype=F32) / ROPE_HALF)
    ang = pos[:, None] * inv_freq[None, :]
    cos, sin = jnp.cos(ang), jnp.sin(ang)
    t = pos.shape[0]
    pad1 = jnp.ones((t, 64 - 2 * ROPE_HALF), F32)
    pad0 = jnp.zeros((t, 64 - 2 * ROPE_HALF), F32)
    c64 = jnp.concatenate([cos, cos, pad1], axis=1)
    s64 = jnp.concatenate([-sin, sin, pad0], axis=1)
    return jnp.concatenate([c64, c64, s64, s64], axis=1)


def _prep_params(g_mix, w_in, mu_shift, w0, w_decay_up, a0, w_iclr_up, w_gate_up, k_k, k_a, r_k, gn_w, gn_b,
                 q_norm, k_norm, w_a_proj, w_b_proj, w_out, g_ffn, w_grp, b_grp, w_exp, b_exp, w_up, w_gate,
                 w_down, g_ple, w_ple_gate, b_ple_gate, w_ple_proj):
    row = lambda v: v.reshape(1, -1).astype(F32)
    n_kw = IDX_DIM + IDX_HEADS
    w_pad = jnp.concatenate([w_in[:, :C_KW + n_kw], jnp.zeros((D_MODEL, LANES - n_kw), F32),
                             w_in[:, C_KW + n_kw:]], axis=1).astype(BF16)
    z = jnp.zeros((64, A_WIDTH), F32)
    lora = jnp.concatenate([jnp.concatenate([w_decay_up, z], axis=1),
                            jnp.concatenate([z, w_iclr_up], axis=1)], axis=0)
    lora_hi, lora_lo = _hi_lo(lora)
    n_r = N_GROUPS + N_EXPERTS
    w_router = jnp.concatenate([w_grp, w_exp, jnp.zeros((D_MODEL, LANES - n_r), F32)], axis=1)
    wr_hi, wr_lo = _hi_lo(w_router)
    b_router = jnp.concatenate([b_grp, b_exp, jnp.zeros((LANES - n_r,), F32)]).reshape(1, LANES)
    idx = np.arange(LANES)
    ones_blk = jnp.asarray((idx[:, None] // 64) == (idx[None, :] // 64), BF16)
    et = jnp.asarray(np.arange(64)[:, None] == (np.arange(A_WIDTH)[None, :] % 64), BF16)
    return dict(
        g_mix=row(g_mix), w_pad=w_pad, qn=row(jnp.tile(q_norm, 2)), kn=row(jnp.tile(k_norm, 2)),
        ones_blk=ones_blk, et=et,
        mu=row(mu_shift), w0=row(w0), lora_hi=lora_hi, lora_lo=lora_lo, a0=row(a0), wg=w_gate_up.astype(BF16),
        k_k=row(k_k), k_a=row(k_a), r_k=row(r_k), gn_w=row(gn_w), gn_b=row(gn_b),
        wa=w_a_proj.astype(BF16), wb=w_b_proj.astype(BF16), wo=w_out.astype(BF16), g_ffn=row(g_ffn),
        wr_hi=wr_hi, wr_lo=wr_lo, b_router=b_router,
        w_gate=w_gate.astype(BF16), w_up=w_up.astype(BF16), w_down=w_down.astype(BF16),
        g_ple=row(g_ple), w_ple_gate=w_ple_gate.astype(BF16), b_ple_gate=row(b_ple_gate),
        w_ple_proj=w_ple_proj.astype(BF16))


def _tile_rows(n, pref):
    t = pref
    while n % t:
        t //= 2
    return t


def _state_in(wkv):
    b = wkv.shape[0]
    return jnp.transpose(wkv, (0, 2, 1, 3)).reshape(b, 64, A_WIDTH)


def _state_out(s):
    b = s.shape[0]
    return jnp.transpose(s.reshape(b, 64, A_HEADS, 64), (0, 2, 1, 3))


def _layer_front(x, pos, p, tm):
    bsz, t, _ = x.shape
    n = bsz * t
    tab = _rope_table(pos)
    if t % tm:
        tab = jnp.tile(tab, (tm // t, 1))
    outs = _inproj(x.reshape(n, D_MODEL), tab, p["g_mix"], p["w_pad"], p["qn"], p["kn"], p["ones_blk"], tm)
    return outs


def _layer_back(x, pemb, o_a, o_b, ga, gb, p):
    n = x.shape[0]
    x1, h2, gate = _merge(x, o_a, o_b, ga, gb, p, _tile_rows(n, 256))
    return _moe(h2, gate, x1, pemb, p, _tile_rows(n, 512))


def kernel(x_prompt, x_sample, p_prompt, p_sample, cache_k, cache_v, cache_k_idx, state_wkv, state_shift,
           page_table, g_mix, w_in, mu_shift, w0, w_decay_up, a0, w_iclr_up, w_gate_up, k_k, k_a, r_k, gn_w,
           gn_b, q_norm, k_norm, w_a_proj, w_b_proj, w_out, g_ffn, w_grp, b_grp, w_exp, b_exp, w_up, w_gate,
           w_down, g_ple, w_ple_gate, b_ple_gate, w_ple_proj):
    depth = g_mix.shape[0]
    assert depth == 1
    bsz, seq, _ = x_prompt.shape
    dbsz, ds, _ = x_sample.shape
    n_pages = page_table.shape[1]
    past = n_pages * PAGE_SIZE
    assert seq % KEY_GROUP == 0 and past % KEY_GROUP == 0 and ds <= SAMPLE_ROWS
    i = 0
    p = _prep_params(g_mix[i], w_in[i], mu_shift[i], w0[i], w_decay_up[i], a0[i], w_iclr_up[i], w_gate_up[i],
                     k_k[i], k_a[i], r_k[i], gn_w[i], gn_b[i], q_norm[i], k_norm[i], w_a_proj[i], w_b_proj[i],
                     w_out[i], g_ffn[i], w_grp[i], b_grp[i], w_exp[i], b_exp[i], w_up[i], w_gate[i], w_down[i],
                     g_ple[i], w_ple_gate[i], b_ple_gate[i], w_ple_proj[i])

    n_p = bsz * seq
    tm_p = _tile_rows(seq, 256)
    za, q, k, v, kb, vb, qi, ki, kib, wi, ga, gb = _layer_front(x_prompt, jnp.arange(seq, dtype=F32), p, tm_p)
    tb = _tile_rows(seq, 128)
    o_a, s_fin = _rwkv(za.reshape(bsz, seq, A_COLS), jnp.zeros((bsz, 1, A_COLS), F32),
                       jnp.zeros((bsz, 64, A_WIDTH), F32), p, tb, tb)
    wt = jnp.transpose(wi.reshape(bsz, seq, LANES)[:, :, IDX_DIM:IDX_DIM + IDX_HEADS], (0, 2, 1))
    o_b = _dsa_prompt(q.reshape(bsz, seq, 512), qi.reshape(bsz, seq, 256), wt, kb.reshape(bsz, seq, 128),
                      vb.reshape(bsz, seq, 128), kib.reshape(bsz, seq, 128))
    y_p = _layer_back(x_prompt.reshape(n_p, D_MODEL), p_prompt[i].reshape(n_p, PLE_DIM),
                      o_a.reshape(n_p, A_WIDTH), o_b.reshape(n_p, 512), ga, gb, p)
    k_prompt = k.reshape(1, bsz, seq, B_KV_HEADS, B_HEAD_DIM)
    v_prompt = v.reshape(1, bsz, seq, B_KV_HEADS, B_HEAD_DIM)
    ki_prompt = ki.reshape(1, bsz, seq, IDX_DIM)
    wkv_prompt = _state_out(s_fin)[None]
    shift_prompt = za.reshape(bsz, seq, A_COLS)[:, -1][None]

    n_s = dbsz * ds
    pos_s = past + jnp.arange(ds, dtype=F32)
    za, q, k, v, kb, vb, qi, ki, kib, wi, ga, gb = _layer_front(x_sample, pos_s, p, n_s)
    c = WKV_CHUNK
    za3 = jnp.pad(za.reshape(dbsz, ds, A_COLS), ((0, 0), (0, c - ds), (0, 0)))
    o_a, s_fin = _rwkv(za3, state_shift[i].reshape(dbsz, 1, A_COLS), _state_in(state_wkv[i]), p, c, ds)
    o_a = o_a[:, :ds]
    pad_rows = lambda u, r: jnp.pad(u.reshape(dbsz, ds, -1), ((0, 0), (0, r - ds), (0, 0)))
    n_pool = cache_k.shape[1]
    o_b = _dsa_sample(page_table, pad_rows(q, SAMPLE_ROWS), pad_rows(qi, SAMPLE_ROWS), pad_rows(wi, SAMPLE_ROWS),
                      pad_rows(kb, LANES), pad_rows(vb, LANES), pad_rows(kib, LANES),
                      cache_k[i].reshape(n_pool, PAGE_SIZE, 128), cache_v[i].reshape(n_pool, PAGE_SIZE, 128),
                      cache_k_idx[i], ds)[:, :ds]
    y_s = _layer_back(x_sample.reshape(n_s, D_MODEL), p_sample[i].reshape(n_s, PLE_DIM),
                      o_a.reshape(n_s, A_WIDTH), o_b.reshape(n_s, 512), ga, gb, p)
    k_sample = k.reshape(1, dbsz, ds, B_KV_HEADS, B_HEAD_DIM)
    v_sample = v.reshape(1, dbsz, ds, B_KV_HEADS, B_HEAD_DIM)
    ki_sample = ki.reshape(1, dbsz, ds, IDX_DIM)
    wkv_sample = _state_out(s_fin)[None]
    shift_sample = za.reshape(dbsz, ds, A_COLS)[:, -1][None]

    return (y_p.reshape(bsz, seq, D_MODEL), y_s.reshape(dbsz, ds, D_MODEL),
            k_prompt, v_prompt, ki_prompt, wkv_prompt, shift_prompt,
            k_sample, v_sample, ki_sample, wkv_sample, shift_sample)
```
